```python
import math
import jax, jax.numpy as jnp
from jax import lax
import numpy as np

D_MODEL = 2048
BATCH = 16
SEQ = 2048
DEPTH = 1

RMS_EPS = 1e-6
RWKV_WIDTH = D_MODEL // 2
RWKV_HEAD = 64
RWKV_HEADS = RWKV_WIDTH // RWKV_HEAD
DECAY_RANK = 64
AAA_RANK = 64
GATE_RANK = 160
RWKV_GN_EPS = 64e-5
S5_WIDTH = D_MODEL // 2
S5_GROUP = 16
S5_GROUPS = S5_WIDTH // S5_GROUP
S5_STATE = 64
DT_MIN = 1e-3
DT_MAX = 1e-1
N_GROUPS = 4
EXPERTS_PER_GROUP = 8
N_EXPERTS = N_GROUPS * EXPERTS_PER_GROUP
TOP_K_INNER = 2
D_EXPERT = D_MODEL // 4
RWKV_SPLITS = [RWKV_WIDTH, 2 * RWKV_WIDTH, 3 * RWKV_WIDTH, 3 * RWKV_WIDTH + DECAY_RANK, 3 * RWKV_WIDTH + DECAY_RANK + AAA_RANK]
RWKV_COLS = 3 * RWKV_WIDTH + DECAY_RANK + AAA_RANK + GATE_RANK
IN_SPLITS = [RWKV_COLS, RWKV_COLS + S5_WIDTH, RWKV_COLS + S5_WIDTH + D_MODEL]
IN_COLS = RWKV_COLS + S5_WIDTH + 2 * D_MODEL

kernel_name = "hybrid_rwkv7_s5_hmoe_block"


def rms_norm(x, g):
    xf = x.astype(jnp.float32)
    y = xf * lax.rsqrt(jnp.mean(xf * xf, axis=-1, keepdims=True) + RMS_EPS)
    return (y * g.astype(jnp.float32)).astype(x.dtype)


def token_shift(p, mu):
    prev = jnp.pad(p, ((0, 0), (1, 0), (0, 0)))[:, :-1]
    return p + (prev - p) * mu


def rwkv7_recurrence(r, w, k, v, a, b):
    Bn, T, H, N = r.shape
    seq = tuple(jnp.moveaxis(t, 1, 0) for t in (r, w, k, v, a, b))

    def step(S, inp):
        rt, wt, kt, vt, at, bt = inp
        sa = jnp.einsum('bhij,bhj->bhi', S, at)
        S = S * wt[:, :, None, :] + sa[..., None] * bt[:, :, None, :] + vt[..., None] * kt[:, :, None, :]
        y = jnp.einsum('bhij,bhj->bhi', S, rt)
        return S, y

    S0 = jnp.zeros((Bn, H, N, N), jnp.float32)
    _, ys = lax.scan(step, S0, seq)
    return jnp.moveaxis(ys, 0, 1)


def rwkv7_branch(p, mu, w0, w_up, a0, a_up, g_up, k_k, k_a, r_k, ln_w, ln_b, w_out):
    f32 = jnp.float32
    Bn, T, _ = p.shape
    heads = lambda t: t.reshape(Bn, T, RWKV_HEADS, RWKV_HEAD)
    p = token_shift(p, mu)
    r, k, v, xw, xa, xg = jnp.split(p, RWKV_SPLITS, axis=-1)
    w_log = -jax.nn.softplus(-(w0 + jnp.tanh(xw) @ w_up).astype(f32)) - 0.5
    decay = jnp.exp(-jnp.exp(w_log))
    a = jax.nn.sigmoid((a0 + xa @ a_up).astype(f32))
    g = jax.nn.sigmoid(xg) @ g_up
    k = k.astype(f32)
    kk = heads(k * k_k.astype(f32))
    kk = kk / jnp.maximum(jnp.linalg.norm(kk, axis=-1, keepdims=True), 1e-12)
    k = k * (1.0 + (a - 1.0) * k_a.astype(f32))
    rh, kh, vh = heads(r.astype(f32)), heads(k), heads(v.astype(f32))
    y = rwkv7_recurrence(rh, heads(decay), kh, vh, -kk, kk * heads(a))
    mean = jnp.mean(y, axis=-1, keepdims=True)
    var = jnp.mean(jnp.square(y - mean), axis=-1, keepdims=True)
    y = ((y - mean) * lax.rsqrt(var + RWKV_GN_EPS)).reshape(Bn, T, RWKV_WIDTH)
    y = y * ln_w.astype(f32) + ln_b.astype(f32)
    bonus = jnp.sum(rh * kh * r_k.astype(f32), axis=-1, keepdims=True) * vh
    y = y + bonus.reshape(Bn, T, RWKV_WIDTH)
    return (y * g.astype(f32)).astype(p.dtype) @ w_out


def s5_branch(u, lam_re, lam_im, log_step, b_re, b_im, c_re, c_im, d_skip, w_glu_v, w_glu_g):
    f32 = jnp.float32
    Bn, T, _ = u.shape
    uf = u.astype(f32).reshape(Bn, T, S5_GROUPS, S5_GROUP)
    lam = lax.complex(lam_re.astype(f32), lam_im.astype(f32))
    step = jnp.exp(log_step.astype(f32))[:, None]
    a_bar = jnp.exp(lam * step)
    b_bar = ((a_bar - 1.0) / lam)[..., None] * lax.complex(b_re.astype(f32), b_im.astype(f32))
    bu = lax.complex(jnp.einsum('btgh,gph->btgp', uf, jnp.real(b_bar)),
                     jnp.einsum('btgh,gph->btgp', uf, jnp.imag(b_bar)))
    a_elems = jnp.broadcast_to(a_bar, bu.shape)

    def combine(left, right):
        a_l, b_l = left
        a_r, b_r = right
        return a_r * a_l, a_r * b_l + b_r

    _, xs = lax.associative_scan(combine, (a_elems, bu), axis=1)
    y = (jnp.einsum('btgp,ghp->btgh', jnp.real(xs), c_re.astype(f32))
         - jnp.einsum('btgp,ghp->btgh', jnp.imag(xs), c_im.astype(f32))
         + d_skip.astype(f32).reshape(S5_GROUPS, S5_GROUP) * uf)
    z = jax.nn.gelu(y.reshape(Bn, T, S5_WIDTH)).astype(u.dtype)
    return (z @ w_glu_v) * jax.nn.sigmoid(z @ w_glu_g)


def hierarchical_moe(h, wr_group, br_group, wr_expert, br_expert, w_gate, w_up, w_down):
    f32 = jnp.float32
    Bn, T, D = h.shape
    n_tok = Bn * T
    ht = h.reshape(n_tok, D)
    group_prob = jax.nn.softmax((ht @ wr_group).astype(f32) + br_group.astype(f32), axis=-1)
    g_prob, g_idx = lax.top_k(group_prob, 1)
    expert_logits = ((ht @ wr_expert).astype(f32) + br_expert.astype(f32)).reshape(n_tok, N_GROUPS, EXPERTS_PER_GROUP)
    sel = expert_logits[jnp.arange(n_tok), g_idx[:, 0]]
    top_logit, top_idx = lax.top_k(sel, TOP_K_INNER)
    weights = jax.nn.softmax(top_logit, axis=-1) * g_prob
    expert_id = g_idx * EXPERTS_PER_GROUP + top_idx
    gates = jnp.sum(jax.nn.one_hot(expert_id, N_EXPERTS, dtype=f32) * weights[..., None], axis=1)
    out = jnp.zeros((n_tok, D), f32)
    for e in range(N_EXPERTS):
        hid = jax.nn.silu(ht @ w_gate[e]) * (ht @ w_up[e])
        out = out + gates[:, e:e + 1] * (hid @ w_down[e]).astype(f32)
    return out.astype(h.dtype).reshape(Bn, T, D)


def setup_inputs(seed: int = 0) -> dict:
    key = jax.random.key(seed)
    ks = jax.random.split(key, 40)
    f32 = jnp.float32
    L, D, W, H, N = DEPTH, D_MODEL, RWKV_WIDTH, RWKV_HEADS, RWKV_HEAD
    G, Hs, P = S5_GROUPS, S5_GROUP, S5_STATE

    def nrm(i, shape, scale):
        return scale * jax.random.normal(ks[i], shape, f32)

    ramp = jnp.arange(W, dtype=f32) / (W - 1)
    w0_init = -6.5 + 5.0 * ramp ** 0.85
    lam_im_init = jnp.pi * jnp.arange(P, dtype=f32)
    return {
        "x": nrm(0, (BATCH, SEQ, D), 1.0),
        "norm_mix_g": 1.0 + nrm(1, (L, D), 0.02),
        "w_in": nrm(2, (L, D, IN_COLS), D ** -0.5),
        "rwkv_mu": 0.5 + nrm(3, (L, RWKV_COLS), 0.1),
        "rwkv_w0": w0_init + nrm(4, (L, W), 0.05),
        "rwkv_w_up": nrm(5, (L, DECAY_RANK, W), 0.1 * DECAY_RANK ** -0.5),
        "rwkv_a0": nrm(6, (L, W), 0.1),
        "rwkv_a_up": nrm(7, (L, AAA_RANK, W), 0.1 * AAA_RANK ** -0.5),
        "rwkv_g_up": nrm(8, (L, GATE_RANK, W), GATE_RANK ** -0.5),
        "rwkv_k_k": 0.85 + nrm(9, (L, W), 0.02),
        "rwkv_k_a": 1.0 + nrm(10, (L, W), 0.02),
        "rwkv_r_k": nrm(11, (L, H, N), 0.1),
        "rwkv_ln_w": 1.0 + nrm(12, (L, W), 0.02),
        "rwkv_ln_b": nrm(13, (L, W), 0.02),
        "rwkv_w_out": nrm(14, (L, W, D), W ** -0.5),
        "s5_lam_re": -0.5 + nrm(15, (L, G, P), 0.01),
        "s5_lam_im": lam_im_init + nrm(16, (L, G, P), 0.01),
        "s5_log_step": jax.random.uniform(ks[17], (L, G), f32, math.log(DT_MIN), math.log(DT_MAX)),
        "s5_b_re": nrm(18, (L, G, P, Hs), (2 * Hs) ** -0.5),
        "s5_b_im": nrm(19, (L, G, P, Hs), (2 * Hs) ** -0.5),
        "s5_c_re": nrm(20, (L, G, Hs, P), (2 * P) ** -0.5),
        "s5_c_im": nrm(21, (L, G, Hs, P), (2 * P) ** -0.5),
        "s5_d": nrm(22, (L, S5_WIDTH), 1.0),
        "s5_w_glu_v": nrm(23, (L, S5_WIDTH, D), S5_WIDTH ** -0.5),
        "s5_w_glu_g": nrm(24, (L, S5_WIDTH, D), S5_WIDTH ** -0.5),
        "w_out": nrm(25, (L, D, D), D ** -0.5),
        "norm_ffn_g": 1.0 + nrm(26, (L, D), 0.02),
        "router_group_w": nrm(27, (L, D, N_GROUPS), D ** -0.5),
        "router_group_b": nrm(28, (L, N_GROUPS), 0.01),
        "router_expert_w": nrm(29, (L, D, N_EXPERTS), D ** -0.5),
        "router_expert_b": nrm(30, (L, N_EXPERTS), 0.01),
        "moe_w_gate": nrm(31, (L, N_EXPERTS, D, D_EXPERT), D ** -0.5),
        "moe_w_up": nrm(32, (L, N_EXPERTS, D, D_EXPERT), D ** -0.5),
        "moe_w_down": nrm(33, (L, N_EXPERTS, D_EXPERT, D), D_EXPERT ** -0.5),
        "norm_final_g": 1.0 + nrm(34, (D,), 0.02),
    }


def reference(x, norm_mix_g, w_in, rwkv_mu, rwkv_w0, rwkv_w_up, rwkv_a0, rwkv_a_up, rwkv_g_up,
              rwkv_k_k, rwkv_k_a, rwkv_r_k, rwkv_ln_w, rwkv_ln_b, rwkv_w_out,
              s5_lam_re, s5_lam_im, s5_log_step, s5_b_re, s5_b_im, s5_c_re, s5_c_im, s5_d,
              s5_w_glu_v, s5_w_glu_g, w_out, norm_ffn_g, router_group_w, router_group_b,
              router_expert_w, router_expert_b, moe_w_gate, moe_w_up, moe_w_down, norm_final_g):
    for l in range(DEPTH):
        h = rms_norm(x, norm_mix_g[l])
        proj = h @ w_in[l]
        p_rwkv, u, gate_a, gate_b = jnp.split(proj, IN_SPLITS, axis=-1)
        y_a = rwkv7_branch(p_rwkv, rwkv_mu[l], rwkv_w0[l], rwkv_w_up[l], rwkv_a0[l], rwkv_a_up[l],
                           rwkv_g_up[l], rwkv_k_k[l], rwkv_k_a[l], rwkv_r_k[l], rwkv_ln_w[l],
                           rwkv_ln_b[l], rwkv_w_out[l])
        y_b = s5_branch(u, s5_lam_re[l], s5_lam_im[l], s5_log_step[l], s5_b_re[l], s5_b_im[l],
                        s5_c_re[l], s5_c_im[l], s5_d[l], s5_w_glu_v[l], s5_w_glu_g[l])
        mixed = jax.nn.sigmoid(gate_a) * y_a + jax.nn.sigmoid(gate_b) * y_b
        x = x + mixed @ w_out[l]
        x = x + hierarchical_moe(rms_norm(x, norm_ffn_g[l]), router_group_w[l], router_group_b[l],
                                 router_expert_w[l], router_expert_b[l], moe_w_gate[l],
                                 moe_w_up[l], moe_w_down[l])
    return rms_norm(x, norm_final_g)
```

```python
import functools
import math

import jax
import jax.numpy as jnp
from jax import lax
from jax.experimental import pallas as pl
from jax.experimental.pallas import tpu as pltpu

F32 = jnp.float32
BF16 = jnp.bfloat16

D_MODEL = 2048
RMS_EPS = 1e-6
RWKV_WIDTH = 1024
RWKV_HEAD = 64
DECAY_RANK = 64
AAA_RANK = 64
GATE_RANK = 160
RWKV_GN_EPS = 64e-5
S5_WIDTH = 1024
S5_GROUP = 16
S5_GROUPS = 64
S5_STATE = 64
N_GROUPS = 4
EXPERTS_PER_GROUP = 8
N_EXPERTS = 32
D_EXPERT = 512

LANES = 128
HEADS_PER_TILE = LANES // RWKV_HEAD
N_PAIRS = RWKV_WIDTH // LANES
CHUNK = 64
S5_CHUNK = 64
S5_TILES = S5_WIDTH // LANES
S5_TILE_STATE = (LANES // S5_GROUP) * S5_STATE
VMEM_LIMIT = 56 * 1024 * 1024

COL_U = 3 * RWKV_WIDTH
COL_LORA = COL_U + S5_WIDTH
LORA_PAD = 512
COL_GA = COL_LORA + LORA_PAD
COL_GB = COL_GA + D_MODEL
PACKED_COLS = COL_GB + D_MODEL


def _params(sem):
    return pltpu.CompilerParams(dimension_semantics=sem, vmem_limit_bytes=VMEM_LIMIT)


def _sigmoid(x):
    return 1.0 / (1.0 + jnp.exp(-x))


def _dot(a, b):
    return jnp.dot(a, b, preferred_element_type=F32)


def _dot_nt(a, b):
    return lax.dot_general(a, b, (((1,), (1,)), ((), ())), preferred_element_type=F32)


def _rmsnorm_kernel(x_ref, g_ref, o_ref):
    x = x_ref[...]
    y = x * lax.rsqrt(jnp.mean(x * x, axis=-1, keepdims=True) + RMS_EPS)
    o_ref[...] = (y * g_ref[...]).astype(o_ref.dtype)


def rmsnorm(x, g, out_dtype, tm=512):
    n, d = x.shape
    return pl.pallas_call(
        _rmsnorm_kernel,
        grid=(n // tm,),
        in_specs=[pl.BlockSpec((tm, d), lambda i: (i, 0)), pl.BlockSpec((1, d), lambda i: (0, 0))],
        out_specs=pl.BlockSpec((tm, d), lambda i: (i, 0)),
        out_shape=jax.ShapeDtypeStruct((n, d), out_dtype),
        compiler_params=_params(("parallel",)),
        name="rmsnorm",
    )(x, g.reshape(1, d))


def _mm_kernel(a_ref, b_ref, o_ref):
    o_ref[...] = _dot(a_ref[...], b_ref[...]).astype(o_ref.dtype)


def _mm_res_kernel(a_ref, b_ref, r_ref, o_ref):
    o_ref[...] = (r_ref[...] + _dot(a_ref[...], b_ref[...])).astype(o_ref.dtype)


def matmul(a, b, out_dtype, tm, tn, residual=None, name="matmul"):
    m, k = a.shape
    _, n = b.shape
    in_specs = [pl.BlockSpec((tm, k), lambda i, j: (i, 0)), pl.BlockSpec((k, tn), lambda i, j: (0, j))]
    args = [a, b]
    kern = _mm_kernel
    if residual is not None:
        in_specs.append(pl.BlockSpec((tm, tn), lambda i, j: (i, j)))
        args.append(residual)
        kern = _mm_res_kernel
    return pl.pallas_call(
        kern,
        grid=(m // tm, n // tn),
        in_specs=in_specs,
        out_specs=pl.BlockSpec((tm, tn), lambda i, j: (i, j)),
        out_shape=jax.ShapeDtypeStruct((m, n), out_dtype),
        compiler_params=_params(("parallel", "arbitrary")),
        name=name,
    )(*args)


def _token_shift(p, prev_row, mu):
    rolled = pltpu.roll(p, 1, 0)
    rid = lax.broadcasted_iota(jnp.int32, p.shape, 0)
    prev = jnp.where(rid == 0, prev_row, rolled)
    return p + (prev - p) * mu


def _rwkv_chunk(r, k, v, ld, a, g, prm, m_state):
    c2 = 2 * CHUNK
    k_k, k_a, r_k, ln_w, ln_b = prm
    lane = lax.broadcasted_iota(jnp.int32, (CHUNK, LANES), 1)
    m0 = lane < RWKV_HEAD

    def head_sum(x):
        s0 = jnp.sum(jnp.where(m0, x, 0.0), axis=-1, keepdims=True)
        s1 = jnp.sum(jnp.where(m0, 0.0, x), axis=-1, keepdims=True)
        return jnp.where(m0, s0, s1)

    kk = k * k_k
    kk = kk / jnp.maximum(jnp.sqrt(head_sum(kk * kk)), 1e-12)
    kmod = k * (1.0 + (a - 1.0) * k_a)
    avec = -kk
    bvec = kk * a

    ti = lax.broadcasted_iota(jnp.int32, (CHUNK, CHUNK), 0)
    tj = lax.broadcasted_iota(jnp.int32, (CHUNK, CHUNK), 1)
    tril = jnp.where(ti >= tj, 1.0, 0.0).astype(F32)
    cum = jnp.dot(tril, ld, preferred_element_type=F32, precision=lax.Precision.HIGHEST)
    cl = cum[CHUNK - 1:CHUNK, :]
    e_pos = jnp.exp(cum)
    e_neg = jnp.exp(-cum)
    e_prev = jnp.exp(cum - ld)
    e_rest = jnp.exp(cl - cum)

    def stack(x):
        return jnp.concatenate([jnp.where(m0, x, 0.0), jnp.where(m0, 0.0, x)], axis=0)

    rt = stack(r * e_pos)
    at_b = stack(avec * e_prev).astype(BF16)
    bt_b = stack(bvec * e_neg).astype(BF16)
    kt_b = stack(kmod * e_neg).astype(BF16)
    v_b = stack(v).astype(BF16)
    bg_t = stack(bvec * e_rest).T.astype(BF16)
    kg_t = stack(kmod * e_rest).T.astype(BF16)
    rt_b = rt.astype(BF16)

    i2 = lax.broadcasted_iota(jnp.int32, (c2, c2), 0)
    j2 = lax.broadcasted_iota(jnp.int32, (c2, c2), 1)
    ti2 = i2 & (CHUNK - 1)
    dij = i2 - j2
    strict = (dij > 0) & (dij <= ti2)
    incl = (dij >= 0) & (dij <= ti2)

    a_ab = jnp.where(strict, _dot_nt(at_b, bt_b), 0.0)
    a_ak = jnp.where(strict, _dot_nt(at_b, kt_b), 0.0)
    a_rb = jnp.where(incl, _dot_nt(rt_b, bt_b), 0.0).astype(BF16)
    a_rk = jnp.where(incl, _dot_nt(rt_b, kt_b), 0.0).astype(BF16)

    eye = jnp.where(i2 == j2, 1.0, 0.0).astype(F32)
    t_inv = eye + a_ab
    a_pow = a_ab
    for _ in range(int(math.log2(CHUNK)) - 1):
        a_pow_b = a_pow.astype(BF16)
        a_pow = _dot(a_pow_b, a_pow_b)
        t_inv = t_inv + _dot(t_inv.astype(BF16), a_pow.astype(BF16))
    t_b = t_inv.astype(BF16)

    at_p = _dot(t_b, at_b).astype(BF16)
    u0 = _dot(t_b, _dot(a_ak.astype(BF16), v_b).astype(BF16)).astype(BF16)
    r_hat = rt + _dot(a_rb, at_p)
    y0 = _dot(a_rb, u0) + _dot(a_rk, v_b)
    p_mat = jnp.where(i2 == j2, jnp.exp(cl), 0.0) + _dot(bg_t, at_p)
    q_mat = _dot(bg_t, u0) + _dot(kg_t, v_b)

    m_b = m_state.astype(BF16)
    ys = _dot(r_hat.astype(BF16), m_b) + y0
    m_new = _dot(p_mat.astype(BF16), m_b) + q_mat

    rowhead = lax.broadcasted_iota(jnp.int32, (c2, LANES), 0) >= CHUNK
    lanehead = lax.broadcasted_iota(jnp.int32, (c2, LANES), 1) >= RWKV_HEAD
    mh = rowhead == lanehead
    inv_n = 1.0 / RWKV_HEAD
    mean = jnp.sum(jnp.where(mh, ys, 0.0), axis=-1, keepdims=True) * inv_n
    dev = jnp.where(mh, ys - mean, 0.0)
    var = jnp.sum(dev * dev, axis=-1, keepdims=True) * inv_n
    yn = dev * lax.rsqrt(var + RWKV_GN_EPS)
    yn = yn[0:CHUNK, :] + yn[CHUNK:c2, :]
    y = yn * ln_w + ln_b
    bonus = head_sum(r * kmod * r_k) * v
    return (y + bonus) * g, m_new


def _rwkv_kernel(r_ref, k_ref, v_ref, l_ref, pr_ref, mul_ref, ww_ref, wa_ref, wg_ref, o_ref,
                 carry, carry_l, m_s):
    tt = r_ref.shape[0]

    @pl.when(pl.program_id(2) == 0)
    def _():
        carry[...] = jnp.zeros_like(carry)
        carry_l[...] = jnp.zeros_like(carry_l)
        m_s[...] = jnp.zeros_like(m_s)

    pr = pr_ref[...]

    def row(i):
        return pr[i:i + 1, :]

    p_r, p_k, p_v, p_l = r_ref[...], k_ref[...], v_ref[...], l_ref[...]
    r = _token_shift(p_r, carry[0:1, :], row(0))
    k = _token_shift(p_k, carry[1:2, :], row(1))
    v = _token_shift(p_v, carry[2:3, :], row(2))
    lo = _token_shift(p_l, carry_l[0:1, :], mul_ref[...])
    carry[0:1, :] = p_r[tt - 1:tt, :]
    carry[1:2, :] = p_k[tt - 1:tt, :]
    carry[2:3, :] = p_v[tt - 1:tt, :]
    carry_l[0:1, :] = p_l[tt - 1:tt, :]

    xw = jnp.tanh(lo[:, 0:LANES]).astype(BF16)
    xa = lo[:, LANES:2 * LANES].astype(BF16)
    xg = _sigmoid(lo[:, 2 * LANES:4 * LANES]).astype(BF16)
    w_in = row(3) + _dot(xw, ww_ref[...])
    neg = -w_in
    softplus = jnp.maximum(neg, 0.0) + jnp.log(1.0 + jnp.exp(-jnp.abs(neg)))
    ld = -jnp.exp(-softplus - 0.5)
    a = _sigmoid(row(4) + _dot(xa, wa_ref[...]))
    g = _dot(xg, wg_ref[...])

    prm = (row(5), row(6), row(7), row(8), row(9))
    m_state = m_s[...]
    for ci in range(tt // CHUNK):
        sl = slice(ci * CHUNK, (ci + 1) * CHUNK)
        out, m_state = _rwkv_chunk(r[sl], k[sl], v[sl], ld[sl], a[sl], g[sl], prm, m_state)
        o_ref[sl, :] = out.astype(o_ref.dtype)
    m_s[...] = m_state


def rwkv_branch(proj, batch, seq, prm_rows, mu_lora, ww, wa, wg, tt=256):
    n = batch * seq
    nt = seq // tt
    row_map = lambda off: (lambda b, hp, t: (b * nt + t, off + hp))
    return pl.pallas_call(
        _rwkv_kernel,
        grid=(batch, N_PAIRS, nt),
        in_specs=[
            pl.BlockSpec((tt, LANES), row_map(0)),
            pl.BlockSpec((tt, LANES), row_map(N_PAIRS)),
            pl.BlockSpec((tt, LANES), row_map(2 * N_PAIRS)),
            pl.BlockSpec((tt, LORA_PAD), lambda b, hp, t: (b * nt + t, COL_LORA // LORA_PAD)),
            pl.BlockSpec((16, LANES), lambda b, hp, t: (0, hp)),
            pl.BlockSpec((1, LORA_PAD), lambda b, hp, t: (0, 0)),
            pl.BlockSpec((LANES, LANES), lambda b, hp, t: (0, hp)),
            pl.BlockSpec((LANES, LANES), lambda b, hp, t: (0, hp)),
            pl.BlockSpec((2 * LANES, LANES), lambda b, hp, t: (0, hp)),
        ],
        out_specs=pl.BlockSpec((tt, LANES), lambda b, hp, t: (b * nt + t, hp)),
        out_shape=jax.ShapeDtypeStruct((n, RWKV_WIDTH), BF16),
        scratch_shapes=[pltpu.VMEM((8, LANES), F32), pltpu.VMEM((8, LORA_PAD), F32),
                        pltpu.VMEM((LANES, LANES), F32)],
        compiler_params=_params(("parallel", "parallel", "arbitrary")),
        name="rwkv7",
    )(proj, proj, proj, proj, prm_rows, mu_lora, ww, wa, wg)


def _gelu_tanh(x):
    return x * (0.5 * (1.0 + jnp.tanh(math.sqrt(2.0 / math.pi) * (x + 0.044715 * (x * x * x)))))


def _s5_kernel(u_ref, b_ref, er_ref, ei_ref, fr_ref, fi_ref, cr_ref, ci_ref, d_ref, o_ref, xr_s, xi_s):
    tt = u_ref.shape[0]
    ns = S5_TILE_STATE
    lc = S5_CHUNK

    @pl.when(pl.program_id(2) == 0)
    def _():
        xr_s[...] = jnp.zeros_like(xr_s)
        xi_s[...] = jnp.zeros_like(xi_s)

    u = u_ref[...]
    bu = _dot(u.astype(BF16), b_ref[0])
    ti = lax.broadcasted_iota(jnp.int32, (lc, lc), 0)
    tj = lax.broadcasted_iota(jnp.int32, (lc, lc), 1)
    tril = jnp.where(ti >= tj, 1.0, 0.0).astype(BF16)
    er, ei, fr, fi = er_ref[...], ei_ref[...], fr_ref[...], fi_ref[...]
    c_re, c_im, d = cr_ref[0], ci_ref[0], d_ref[...]
    xr = xr_s[0:1, :]
    xi = xi_s[0:1, :]
    for s in range(tt // lc):
        sl = slice(s * lc, (s + 1) * lc)
        br = bu[sl, 0:ns]
        bi = bu[sl, ns:2 * ns]
        zr = br * er - bi * ei
        zi = br * ei + bi * er
        sr = _dot(tril, zr.astype(BF16)) + xr
        si = _dot(tril, zi.astype(BF16)) + xi
        x_re = sr * fr - si * fi
        x_im = sr * fi + si * fr
        xr = x_re[lc - 1:lc, :]
        xi = x_im[lc - 1:lc, :]
        y = _dot(x_re.astype(BF16), c_re) - _dot(x_im.astype(BF16), c_im) + d * u[sl, :]
        o_ref[sl, :] = _gelu_tanh(y).astype(o_ref.dtype)
    xr_s[0:1, :] = xr
    xi_s[0:1, :] = xi


def s5_branch(proj, batch, seq, bmat, e_re, e_im, f_re, f_im, c_re, c_im, d_skip, tt=256):
    n = batch * seq
    nt = seq // tt
    ns = S5_TILE_STATE
    tab = pl.BlockSpec((S5_CHUNK, ns), lambda b, j, t: (0, j))
    return pl.pallas_call(
        _s5_kernel,
        grid=(batch, S5_TILES, nt),
        in_specs=[
            pl.BlockSpec((tt, LANES), lambda b, j, t: (b * nt + t, COL_U // LANES + j)),
            pl.BlockSpec((1, LANES, 2 * ns), lambda b, j, t: (j, 0, 0)),
            tab, tab, tab, tab,
            pl.BlockSpec((1, ns, LANES), lambda b, j, t: (j, 0, 0)),
            pl.BlockSpec((1, ns, LANES), lambda b, j, t: (j, 0, 0)),
            pl.BlockSpec((1, LANES), lambda b, j, t: (0, j)),
        ],
        out_specs=pl.BlockSpec((tt, LANES), lambda b, j, t: (b * nt + t, j)),
        out_shape=jax.ShapeDtypeStruct((n, S5_WIDTH), BF16),
        scratch_shapes=[pltpu.VMEM((8, ns), F32), pltpu.VMEM((8, ns), F32)],
        compiler_params=_params(("parallel", "parallel", "arbitrary")),
        name="s5",
    )(proj, bmat, e_re, e_im, f_re, f_im, c_re, c_im, d_skip)


def _s5_tables(lam_re, lam_im, log_step, b_re, b_im, c_re, c_im):
    lam = lax.complex(lam_re.astype(F32), lam_im.astype(F32))
    step = jnp.exp(log_step.astype(F32))[:, None]
    lam_dt = lam * step
    a_bar = jnp.exp(lam_dt)
    b_bar = ((a_bar - 1.0) / lam)[..., None] * lax.complex(b_re.astype(F32), b_im.astype(F32))
    pw = jnp.arange(1, S5_CHUNK + 1, dtype=F32)[:, None, None]
    e_tab = jnp.exp(-pw * lam_dt[None]).reshape(S5_CHUNK, S5_GROUPS * S5_STATE)
    f_tab = jnp.exp(pw * lam_dt[None]).reshape(S5_CHUNK, S5_GROUPS * S5_STATE)
    gpt = LANES // S5_GROUP
    eye = jnp.eye(gpt, dtype=F32)
    bb = jnp.transpose(b_bar, (0, 2, 1)).reshape(S5_TILES, gpt, S5_GROUP, S5_STATE)

    def blockdiag_in(x):
        return jnp.einsum('tghp,gk->tghkp', x, eye).reshape(S5_TILES, LANES, S5_TILE_STATE)

    bmat = jnp.concatenate([blockdiag_in(jnp.real(bb)), blockdiag_in(jnp.imag(bb))], axis=-1).astype(BF16)

    def blockdiag_out(c):
        cc = jnp.transpose(c.astype(F32), (0, 2, 1)).reshape(S5_TILES, gpt, S5_STATE, S5_GROUP)
        return jnp.einsum('tgph,gk->tgpkh', cc, eye).reshape(S5_TILES, S5_TILE_STATE, LANES).astype(BF16)

    return (bmat, jnp.real(e_tab), jnp.imag(e_tab), jnp.real(f_tab), jnp.imag(f_tab),
            blockdiag_out(c_re), blockdiag_out(c_im))


def _merge_kernel(ya_ref, z_ref, ga_ref, gb_ref, wa_ref, wv_ref, wg_ref, o_ref):
    z = z_ref[...]
    y_a = _dot(ya_ref[...], wa_ref[...])
    y_b = _dot(z, wv_ref[...]) * _sigmoid(_dot(z, wg_ref[...]))
    o_ref[...] = (_sigmoid(ga_ref[...]) * y_a + _sigmoid(gb_ref[...]) * y_b).astype(o_ref.dtype)


def merge(ya, z, proj, w_a, w_v, w_g, tm=512, tn=512):
    n, k = ya.shape
    d = w_a.shape[1]
    return pl.pallas_call(
        _merge_kernel,
        grid=(n // tm, d // tn),
        in_specs=[
            pl.BlockSpec((tm, k), lambda i, j: (i, 0)),
            pl.BlockSpec((tm, k), lambda i, j: (i, 0)),
            pl.BlockSpec((tm, tn), lambda i, j: (i, COL_GA // tn + j)),
            pl.BlockSpec((tm, tn), lambda i, j: (i, COL_GB // tn + j)),
            pl.BlockSpec((k, tn), lambda i, j: (0, j)),
            pl.BlockSpec((k, tn), lambda i, j: (0, j)),
            pl.BlockSpec((k, tn), lambda i, j: (0, j)),
        ],
        out_specs=pl.BlockSpec((tm, tn), lambda i, j: (i, j)),
        out_shape=jax.ShapeDtypeStruct((n, d), BF16),
        compiler_params=_params(("parallel", "arbitrary")),
        name="merge",
    )(ya, z, proj, proj, w_a, w_v, w_g)


def _router_kernel(x_ref, g_ref, wr_ref, br_ref, h_ref, gate_ref):
    x = x_ref[...]
    h = (x * lax.rsqrt(jnp.mean(x * x, axis=-1, keepdims=True) + RMS_EPS)) * g_ref[...]
    h_ref[...] = h.astype(h_ref.dtype)
    logits = jnp.dot(h, wr_ref[...], preferred_element_type=F32, precision=lax.Precision.HIGHEST) + br_ref[...]
    lane = lax.broadcasted_iota(jnp.int32, logits.shape, 1)
    neg_inf = -jnp.inf
    big = jnp.int32(LANES)

    def first_argmax(vals):
        m = jnp.max(vals, axis=-1, keepdims=True)
        idx = jnp.min(jnp.where(vals == m, lane, big), axis=-1, keepdims=True)
        return m, idx

    is_group = lane < N_GROUPS
    gl = jnp.where(is_group, logits, neg_inf)
    g_max, g_idx = first_argmax(gl)
    g_prob = 1.0 / jnp.sum(jnp.where(is_group, jnp.exp(gl - g_max), 0.0), axis=-1, keepdims=True)
    lo = N_GROUPS + g_idx * EXPERTS_PER_GROUP
    in_group = (lane >= lo) & (lane < lo + EXPERTS_PER_GROUP)
    el = jnp.where(in_group, logits, neg_inf)
    m1, i1 = first_argmax(el)
    el2 = jnp.where(lane == i1, neg_inf, el)
    m2, i2 = first_argmax(el2)
    e2 = jnp.exp(m2 - m1)
    w1 = g_prob / (1.0 + e2)
    w2 = w1 * e2
    gate_ref[...] = jnp.where(lane == i1, w1, jnp.where(lane == i2, w2, 0.0))


def router(x1, g, wr, br, tm=512):
    n, d = x1.shape
    return pl.pallas_call(
        _router_kernel,
        grid=(n // tm,),
        in_specs=[pl.BlockSpec((tm, d), lambda i: (i, 0)), pl.BlockSpec((1, d), lambda i: (0, 0)),
                  pl.BlockSpec((d, LANES), lambda i: (0, 0)), pl.BlockSpec((1, LANES), lambda i: (0, 0))],
        out_specs=[pl.BlockSpec((tm, d), lambda i: (i, 0)), pl.BlockSpec((tm, LANES), lambda i: (i, 0))],
        out_shape=[jax.ShapeDtypeStruct((n, d), BF16), jax.ShapeDtypeStruct((n, LANES), F32)],
        compiler_params=_params(("parallel",)),
        name="router",
    )(x1, g.reshape(1, d), wr, br)


def _moe_dense_kernel(h_ref, gate_ref, wg_ref, wu_ref, wd_ref, x_ref, gf_ref, o_ref, acc):
    e = pl.program_id(1)

    @pl.when(e == 0)
    def _():
        acc[...] = jnp.zeros_like(acc)

    h = h_ref[...]
    gates = gate_ref[...]
    lane = lax.broadcasted_iota(jnp.int32, gates.shape, 1)
    g_e = jnp.sum(jnp.where(lane == e + N_GROUPS, gates, 0.0), axis=-1, keepdims=True)
    gate_act = _dot(h, wg_ref[0])
    hid = (gate_act * _sigmoid(gate_act)) * _dot(h, wu_ref[0])
    acc[...] += _dot((hid * g_e).astype(BF16), wd_ref[0])

    @pl.when(e == pl.num_programs(1) - 1)
    def _():
        x = x_ref[...] + acc[...]
        y = x * lax.rsqrt(jnp.mean(x * x, axis=-1, keepdims=True) + RMS_EPS)
        o_ref[...] = y * gf_ref[...]


def moe_dense(h2, gates, w_gate, w_up, w_down, x1, g_final, tm=512):
    n, d = h2.shape
    ne, _, de = w_gate.shape
    return pl.pallas_call(
        _moe_dense_kernel,
        grid=(n // tm, ne),
        in_specs=[
            pl.BlockSpec((tm, d), lambda i, e: (i, 0)),
            pl.BlockSpec((tm, LANES), lambda i, e: (i, 0)),
            pl.BlockSpec((1, d, de), lambda i, e: (e, 0, 0)),
            pl.BlockSpec((1, d, de), lambda i, e: (e, 0, 0)),
            pl.BlockSpec((1, de, d), lambda i, e: (e, 0, 0)),
            pl.BlockSpec((tm, d), lambda i, e: (i, 0)),
            pl.BlockSpec((1, d), lambda i, e: (0, 0)),
        ],
        out_specs=pl.BlockSpec((tm, d), lambda i, e: (i, 0)),
        out_shape=jax.ShapeDtypeStruct((n, d), F32),
        scratch_shapes=[pltpu.VMEM((tm, d), F32)],
        compiler_params=_params(("parallel", "arbitrary")),
        name="moe_dense",
    )(h2, gates, w_gate, w_up, w_down, x1, g_final.reshape(1, d))


def _pad_cols(x, width):
    return jnp.pad(x, ((0, 0), (0, width - x.shape[1])))


def _pack_lora_cols(x):
    s1, s2 = DECAY_RANK, DECAY_RANK + AAA_RANK
    return jnp.concatenate([_pad_cols(x[:, :s1], LANES), _pad_cols(x[:, s1:s2], LANES),
                            _pad_cols(x[:, s2:], 2 * LANES)], axis=1)


def _block(x, norm_mix_g, w_in, rwkv_mu, rwkv_w0, rwkv_w_up, rwkv_a0, rwkv_a_up, rwkv_g_up,
           rwkv_k_k, rwkv_k_a, rwkv_r_k, rwkv_ln_w, rwkv_ln_b, rwkv_w_out,
           s5_lam_re, s5_lam_im, s5_log_step, s5_b_re, s5_b_im, s5_c_re, s5_c_im, s5_d,
           s5_w_glu_v, s5_w_glu_g, w_out, norm_ffn_g, router_group_w, router_group_b,
           router_expert_w, router_expert_b, moe_w_gate, moe_w_up, moe_w_down, norm_final_g):
    batch, seq, d = x.shape
    n = batch * seq
    xf = x.reshape(n, d)
    rw = 3 * RWKV_WIDTH
    lora_end = rw + DECAY_RANK + AAA_RANK + GATE_RANK

    w_packed = jnp.concatenate([
        w_in[:, :rw], w_in[:, lora_end:lora_end + S5_WIDTH], _pack_lora_cols(w_in[:, rw:lora_end]),
        w_in[:, lora_end + S5_WIDTH:]], axis=1).astype(BF16)
    mu = rwkv_mu.reshape(1, -1)
    mu_lora = _pack_lora_cols(mu[:, rw:lora_end])
    rows = [mu[0, 0:RWKV_WIDTH], mu[0, RWKV_WIDTH:2 * RWKV_WIDTH], mu[0, 2 * RWKV_WIDTH:rw], rwkv_w0, rwkv_a0,
            rwkv_k_k, rwkv_k_a, rwkv_r_k.reshape(-1), rwkv_ln_w, rwkv_ln_b]
    prm_rows = jnp.concatenate([jnp.stack(rows).astype(F32), jnp.zeros((16 - len(rows), RWKV_WIDTH), F32)], axis=0)
    ww = jnp.pad(rwkv_w_up, ((0, LANES - DECAY_RANK), (0, 0))).astype(BF16)
    wa = jnp.pad(rwkv_a_up, ((0, LANES - AAA_RANK), (0, 0))).astype(BF16)
    wg = jnp.pad(rwkv_g_up, ((0, 2 * LANES - GATE_RANK), (0, 0))).astype(BF16)
    s5_tabs = _s5_tables(s5_lam_re, s5_lam_im, s5_log_step, s5_b_re, s5_b_im, s5_c_re, s5_c_im)
    wr = _pad_cols(jnp.concatenate([router_group_w, router_expert_w], axis=1).astype(F32), LANES)
    br = _pad_cols(jnp.concatenate([router_group_b, router_expert_b]).reshape(1, -1).astype(F32), LANES)

    h = rmsnorm(xf, norm_mix_g, BF16)
    proj = matmul(h, w_packed, F32, tm=1024, tn=512, name="in_proj")
    ya = rwkv_branch(proj, batch, seq, prm_rows, mu_lora, ww, wa, wg)
    z = s5_branch(proj, batch, seq, *s5_tabs, s5_d.reshape(1, -1).astype(F32))
    mixed = merge(ya, z, proj, rwkv_w_out.astype(BF16), s5_w_glu_v.astype(BF16), s5_w_glu_g.astype(BF16))
    x1 = matmul(mixed, w_out.astype(BF16), F32, tm=1024, tn=512, residual=xf, name="out_proj")

    h2, gates = router(x1, norm_ffn_g, wr, br)
    out = moe_dense(h2, gates, moe_w_gate.astype(BF16), moe_w_up.astype(BF16), moe_w_down.astype(BF16),
                    x1, norm_final_g)
    return out.reshape(batch, seq, d)


def kernel(x, norm_mix_g, w_in, rwkv_mu, rwkv_w0, rwkv_w_up, rwkv_a0, rwkv_a_up, rwkv_g_up, rwkv_k_k, rwkv_k_a, rwkv_r_k, rwkv_ln_w, rwkv_ln_b, rwkv_w_out, s5_lam_re, s5_lam_im, s5_log_step, s5_b_re, s5_b_im, s5_c_re, s5_c_im, s5_d, s5_w_glu_v, s5_w_glu_g, w_out, norm_ffn_g, router_group_w, router_group_b, router_expert_w, router_expert_b, moe_w_gate, moe_w_up, moe_w_down, norm_final_g):
    assert norm_mix_g.shape[0] == 1, "single-layer block"
    layer = lambda p: p[0]
    return _block(x, layer(norm_mix_g), layer(w_in), layer(rwkv_mu), layer(rwkv_w0), layer(rwkv_w_up),
                  layer(rwkv_a0), layer(rwkv_a_up), layer(rwkv_g_up), layer(rwkv_k_k), layer(rwkv_k_a),
                  layer(rwkv_r_k), layer(rwkv_ln_w), layer(rwkv_ln_b), layer(rwkv_w_out), layer(s5_lam_re),
                  layer(s5_lam_im), layer(s5_log_step), layer(s5_b_re), layer(s5_b_im), layer(s5_c_re),
                  layer(s5_c_im), layer(s5_d), layer(s5_w_glu_v), layer(s5_w_glu_g), layer(w_out),
                  layer(norm_ffn_g), layer(router_group_w), layer(router_group_b), layer(router_expert_w),
                  layer(router_expert_b), layer(moe_w_gate), layer(moe_w_up), layer(moe_w_down), norm_final_g)
```
